```python
import math
import jax, jax.numpy as jnp
from jax import lax
import numpy as np

D_MODEL = 2048
BATCH = 4
SEQ = 2048
DEPTH = 4

NUM_MIXERS = 2
N_SSD_LAYERS = (DEPTH + 1) // 2
N_POOL_LAYERS = DEPTH // 2
N_META = 16
EPS = 1e-6

SSD_EXPAND = 2
D_INNER = SSD_EXPAND * D_MODEL
SSD_HEAD_DIM = 64
SSD_HEADS = D_INNER // SSD_HEAD_DIM
D_STATE = 128
SSD_GROUPS = 8
HEADS_PER_GROUP = SSD_HEADS // SSD_GROUPS
D_CONV = 4
CHUNK = 256
CONV_DIM = D_INNER + 2 * SSD_GROUPS * D_STATE
D_IN_PROJ = D_INNER + CONV_DIM + SSD_HEADS
DT_MIN = 0.001
DT_MAX = 0.1
A_INIT_MAX = 16.0

POOL_WINDOWS = (2, 4, 8, 16)
POOL_GROUPS = len(POOL_WINDOWS)
POOL_GROUP_DIM = D_MODEL // POOL_GROUPS

FFN_HIDDEN = -(-8 * D_MODEL // (3 * 256)) * 256

kernel_name = 'hybrid_ssd_pool_trunk'


def rmsnorm(x, w):
    xf = x.astype(jnp.float32)
    y = xf * lax.rsqrt(jnp.mean(xf * xf, axis=-1, keepdims=True) + EPS)
    return (y * w).astype(x.dtype)


def causal_depthwise_conv(u, w, b):
    c = u.shape[-1]
    out = lax.conv_general_dilated(
        u, w[:, None, :].astype(u.dtype), window_strides=(1,),
        padding=[(w.shape[0] - 1, 0)], dimension_numbers=('NWC', 'WIO', 'NWC'),
        feature_group_count=c)
    return out + b.astype(u.dtype)


def ssd_chunked(X, dt, A, Bm, Cm):
    bsz, l, _, p = X.shape
    nc = l // CHUNK
    G, R, N = SSD_GROUPS, HEADS_PER_GROUP, D_STATE
    Xc = (X * dt[..., None]).reshape(bsz, nc, CHUNK, G, R, p)
    dA = jnp.moveaxis((dt * A).reshape(bsz, nc, CHUNK, G, R), 2, -1)
    dA_cs = jnp.cumsum(dA, axis=-1)
    Bc = Bm.reshape(bsz, nc, CHUNK, G, N)
    Cc = Cm.reshape(bsz, nc, CHUNK, G, N)
    causal = jnp.tril(jnp.ones((CHUNK, CHUNK), dtype=bool))
    seg = dA_cs[..., :, None] - dA_cs[..., None, :]
    Lmat = jnp.exp(jnp.where(causal, seg, -jnp.inf))
    CB = jnp.einsum('bclgn,bcsgn->bcgls', Cc, Bc)
    y_diag = jnp.einsum('bcgls,bcgrls,bcsgrp->bclgrp', CB, Lmat, Xc)
    decay_states = jnp.exp(dA_cs[..., -1:] - dA_cs)
    states = jnp.einsum('bclgn,bcgrl,bclgrp->bcgrpn', Bc, decay_states, Xc)
    chunk_decay = jnp.exp(dA_cs[..., -1])

    def step(carry, inp):
        st, dec = inp
        return carry * dec[..., None, None] + st, carry

    init = jnp.zeros_like(states[:, 0])
    _, prev_states = lax.scan(step, init, (jnp.moveaxis(states, 1, 0), jnp.moveaxis(chunk_decay, 1, 0)))
    prev_states = jnp.moveaxis(prev_states, 0, 1)
    y_off = jnp.einsum('bclgn,bcgrpn,bcgrl->bclgrp', Cc, prev_states, jnp.exp(dA_cs))
    return (y_diag + y_off).reshape(bsz, l, SSD_HEADS, p)


def ssd_mixer(u, w_in, conv_w, conv_b, dt_bias, a_log, d_skip, norm_w, w_out):
    bsz, L, _ = u.shape
    zxbcdt = u @ w_in
    z, xBC, dt_raw = jnp.split(zxbcdt, [D_INNER, D_INNER + CONV_DIM], axis=-1)
    xBC = jax.nn.silu(causal_depthwise_conv(xBC, conv_w, conv_b))
    xs, Bs, Cs = jnp.split(xBC, [D_INNER, D_INNER + SSD_GROUPS * D_STATE], axis=-1)
    dt = jax.nn.softplus(dt_raw.astype(jnp.float32) + dt_bias.astype(jnp.float32))
    A = -jnp.exp(a_log.astype(jnp.float32))
    Xf = xs.astype(jnp.float32).reshape(bsz, L, SSD_HEADS, SSD_HEAD_DIM)
    Bf = Bs.astype(jnp.float32).reshape(bsz, L, SSD_GROUPS, D_STATE)
    Cf = Cs.astype(jnp.float32).reshape(bsz, L, SSD_GROUPS, D_STATE)
    pad_front = (-N_META) % CHUNK
    pad_back = (-(pad_front + L)) % CHUNK

    def pad(t):
        return jnp.pad(t, ((0, 0), (pad_front, pad_back)) + ((0, 0),) * (t.ndim - 2))

    y = ssd_chunked(pad(Xf), pad(dt), A, pad(Bf), pad(Cf))[:, pad_front:pad_front + L]
    y = y + Xf * d_skip.astype(jnp.float32)[:, None]
    y = y.reshape(bsz, L, D_INNER) * jax.nn.silu(z.astype(jnp.float32))
    yg = y.reshape(bsz, L, SSD_GROUPS, D_INNER // SSD_GROUPS)
    yg = yg * lax.rsqrt(jnp.mean(yg * yg, axis=-1, keepdims=True) + EPS)
    y = (yg.reshape(bsz, L, D_INNER) * norm_w).astype(u.dtype)
    return y @ w_out


def pool_mixer(u, w_group, b, scale):
    bsz, L, _ = u.shape
    uf = u.astype(jnp.float32)
    cs = jnp.concatenate([jnp.zeros((bsz, 1, D_MODEL), jnp.float32), jnp.cumsum(uf, axis=1)], axis=1)
    t = jnp.arange(L)
    pooled = []
    for g, win in enumerate(POOL_WINDOWS):
        cs_g = cs[..., g * POOL_GROUP_DIM:(g + 1) * POOL_GROUP_DIM]
        start = jnp.maximum(t + 1 - win, 0)
        count = (t + 1 - start).astype(jnp.float32)
        pooled.append((cs_g[:, 1:] - cs_g[:, start]) / count[None, :, None])
    pooled = jnp.stack(pooled, axis=2)
    mixed = (pooled - uf.reshape(bsz, L, POOL_GROUPS, POOL_GROUP_DIM)).astype(u.dtype)
    out = jnp.einsum('blgc,gcd->blgd', mixed, w_group) + b.reshape(POOL_GROUPS, POOL_GROUP_DIM)
    return out.reshape(bsz, L, D_MODEL) * scale


def swiglu(u, w_gate, w_up, w_down):
    return (jax.nn.silu(u @ w_gate) * (u @ w_up)) @ w_down


def setup_inputs(seed: int = 0) -> dict:
    key = jax.random.key(seed)
    ks = jax.random.split(key, 17)
    f32 = jnp.float32

    def nrm(k, shape, scale):
        return scale * jax.random.normal(k, shape, f32)

    x = jax.random.normal(ks[0], (BATCH, SEQ, D_MODEL), f32)
    meta_tokens = nrm(ks[1], (N_META, D_MODEL), 1.0)
    norm_w = 1.0 + nrm(ks[2], (DEPTH, 4, D_MODEL), 0.05)
    ssd_w_in = nrm(ks[3], (N_SSD_LAYERS, D_MODEL, D_IN_PROJ), D_MODEL ** -0.5)
    ssd_conv_w = nrm(ks[4], (N_SSD_LAYERS, D_CONV, CONV_DIM), D_CONV ** -0.5)
    ssd_conv_b = nrm(ks[5], (N_SSD_LAYERS, CONV_DIM), 0.01)
    dt0 = jnp.exp(jax.random.uniform(ks[6], (N_SSD_LAYERS, SSD_HEADS), f32,
                                     math.log(DT_MIN), math.log(DT_MAX)))
    ssd_dt_bias = dt0 + jnp.log(-jnp.expm1(-dt0))
    ssd_a_log = jnp.log(jax.random.uniform(ks[7], (N_SSD_LAYERS, SSD_HEADS), f32, 1.0, A_INIT_MAX))
    ssd_d = 1.0 + nrm(ks[8], (N_SSD_LAYERS, SSD_HEADS), 0.1)
    ssd_norm_w = 1.0 + nrm(ks[9], (N_SSD_LAYERS, D_INNER), 0.05)
    ssd_w_out = nrm(ks[10], (N_SSD_LAYERS, D_INNER, D_MODEL), D_INNER ** -0.5)
    pool_w = nrm(ks[11], (N_POOL_LAYERS, POOL_GROUPS, POOL_GROUP_DIM, POOL_GROUP_DIM), POOL_GROUP_DIM ** -0.5)
    pool_b = nrm(ks[12], (N_POOL_LAYERS, D_MODEL), 0.01)
    pool_scale = 1.0 + nrm(ks[13], (N_POOL_LAYERS, D_MODEL), 0.1)
    ffn_w_gate = nrm(ks[14], (DEPTH, D_MODEL, FFN_HIDDEN), D_MODEL ** -0.5)
    ffn_w_up = nrm(ks[15], (DEPTH, D_MODEL, FFN_HIDDEN), D_MODEL ** -0.5)
    ffn_w_down = nrm(ks[16], (DEPTH, FFN_HIDDEN, D_MODEL), FFN_HIDDEN ** -0.5)
    return {'x': x, 'meta_tokens': meta_tokens, 'norm_w': norm_w,
            'ssd_w_in': ssd_w_in, 'ssd_conv_w': ssd_conv_w, 'ssd_conv_b': ssd_conv_b,
            'ssd_dt_bias': ssd_dt_bias, 'ssd_a_log': ssd_a_log, 'ssd_d': ssd_d,
            'ssd_norm_w': ssd_norm_w, 'ssd_w_out': ssd_w_out,
            'pool_w': pool_w, 'pool_b': pool_b, 'pool_scale': pool_scale,
            'ffn_w_gate': ffn_w_gate, 'ffn_w_up': ffn_w_up, 'ffn_w_down': ffn_w_down}


def reference(x, meta_tokens, norm_w, ssd_w_in, ssd_conv_w, ssd_conv_b, ssd_dt_bias,
              ssd_a_log, ssd_d, ssd_norm_w, ssd_w_out, pool_w, pool_b, pool_scale,
              ffn_w_gate, ffn_w_up, ffn_w_down):
    bsz = x.shape[0]
    meta = jnp.broadcast_to(meta_tokens[None].astype(x.dtype), (bsz, N_META, D_MODEL))
    h = jnp.concatenate([meta, x], axis=1)
    for i in range(DEPTH):
        j = i // NUM_MIXERS
        u = rmsnorm(h, norm_w[i, 0])
        if i % NUM_MIXERS == 0:
            mix = ssd_mixer(u, ssd_w_in[j], ssd_conv_w[j], ssd_conv_b[j], ssd_dt_bias[j],
                            ssd_a_log[j], ssd_d[j], ssd_norm_w[j], ssd_w_out[j])
        else:
            mix = pool_mixer(u, pool_w[j], pool_b[j], pool_scale[j])
        h = h + rmsnorm(mix, norm_w[i, 1])
        f = swiglu(rmsnorm(h, norm_w[i, 2]), ffn_w_gate[i], ffn_w_up[i], ffn_w_down[i])
        h = h + rmsnorm(f, norm_w[i, 3])
    return h[:, N_META:]
```

```python
import functools

import jax
import jax.numpy as jnp
from jax import lax
from jax.experimental import pallas as pl
from jax.experimental.pallas import tpu as pltpu

F32 = jnp.float32
BF16 = jnp.bfloat16

D_MODEL = 2048
N_META = 16
EPS = 1e-6

D_INNER = 4096
SSD_HEADS = 64
SSD_HEAD_DIM = 64
D_STATE = 128
SSD_GROUPS = 8
HEADS_PER_GROUP = 8
GROUP_DIM = D_INNER // SSD_GROUPS
D_CONV = 4
ZX_DIM = 2 * D_INNER + 2 * SSD_GROUPS * D_STATE

POOL_WINDOWS = (2, 4, 8, 16)
POOL_GROUP_DIM = 512
FFN_HIDDEN = 5632

LANES = 128
SUBLANES = 8
TM = 688
TN = 512
Q = 256
HALF = 4
VMEM_LIMIT = 56 * 1024 * 1024


def _rms(x, w):
    return x * lax.rsqrt(jnp.mean(x * x, axis=-1, keepdims=True) + EPS) * w


def _silu(x):
    return x * jax.nn.sigmoid(x)


def _params(*sem):
    return pltpu.CompilerParams(dimension_semantics=sem, vmem_limit_bytes=VMEM_LIMIT)


def _inproj_kernel(h_ref, nw_ref, w_ref, wdt_ref, o_ref, odt_ref, u_sc):
    @pl.when(pl.program_id(1) == 0)
    def _():
        u = _rms(h_ref[...], nw_ref[...]).astype(BF16)
        u_sc[...] = u
        odt_ref[...] = jnp.dot(u, wdt_ref[...], preferred_element_type=F32)

    o_ref[...] = jnp.dot(u_sc[...], w_ref[...], preferred_element_type=F32)


def _inproj(h, nw, w, wdt):
    t = h.shape[0]
    return pl.pallas_call(
        _inproj_kernel,
        grid=(t // TM, ZX_DIM // TN),
        in_specs=[
            pl.BlockSpec((TM, D_MODEL), lambda i, j: (i, 0)),
            pl.BlockSpec((1, D_MODEL), lambda i, j: (0, 0)),
            pl.BlockSpec((D_MODEL, TN), lambda i, j: (0, j)),
            pl.BlockSpec((D_MODEL, LANES), lambda i, j: (0, 0)),
        ],
        out_specs=[
            pl.BlockSpec((TM, TN), lambda i, j: (i, j)),
            pl.BlockSpec((TM, LANES), lambda i, j: (i, 0)),
        ],
        out_shape=[
            jax.ShapeDtypeStruct((t, ZX_DIM), F32),
            jax.ShapeDtypeStruct((t, LANES), F32),
        ],
        scratch_shapes=[pltpu.VMEM((TM, D_MODEL), BF16)],
        compiler_params=_params("arbitrary", "arbitrary"),
        name="ssd_inproj",
    )(h, nw, w, wdt)


def _causal_conv_silu(raw_ref, win_sc, w_ref, b_ref, valid):
    win_sc[SUBLANES:SUBLANES + Q, :] = raw_ref[0]
    acc = b_ref[...]
    for k in range(D_CONV):
        off = SUBLANES - (D_CONV - 1) + k
        acc = acc + w_ref[k:k + 1, :] * win_sc[off:off + Q, :]
    win_sc[0:SUBLANES, :] = win_sc[Q:Q + SUBLANES, :]
    return jnp.where(valid, _silu(acc), 0.0)


def _split3(v):
    hi = v.astype(BF16)
    r1 = v - hi.astype(F32)
    mid = r1.astype(BF16)
    lo = (r1 - mid.astype(F32)).astype(BF16)
    return hi, mid, lo


def _scan_kernel(seq_len, z_ref, x_ref, b_ref, c_ref, dt_ref, cwx_ref, cwb_ref, cwc_ref,
                 cbx_ref, cbb_ref, cbc_ref, dtb_ref, alog_ref, dsk_ref, nw_ref,
                 y_ref, st_sc, xw_sc, bw_sc, cw_sc):
    g = pl.program_id(1)
    c = pl.program_id(2)

    @pl.when(c == 0)
    def _():
        st_sc[...] = jnp.zeros_like(st_sc)
        xw_sc[0:SUBLANES, :] = jnp.zeros((SUBLANES, GROUP_DIM), F32)
        bw_sc[0:SUBLANES, :] = jnp.zeros((SUBLANES, D_STATE), F32)
        cw_sc[0:SUBLANES, :] = jnp.zeros((SUBLANES, D_STATE), F32)

    valid = lax.broadcasted_iota(jnp.int32, (Q, 1), 0) < (seq_len - c * Q)

    xs = _causal_conv_silu(x_ref, xw_sc, cwx_ref, cbx_ref, valid)
    bg = _causal_conv_silu(b_ref, bw_sc, cwb_ref, cbb_ref, valid)
    cg = _causal_conv_silu(c_ref, cw_sc, cwc_ref, cbc_ref, valid)

    dt_all = jnp.where(valid, jax.nn.softplus(dt_ref[0] + dtb_ref[...]), 0.0)
    da_all = dt_all * (-jnp.exp(alog_ref[...]))
    shift = (LANES - HEADS_PER_GROUP * g) % LANES
    dt = pltpu.roll(dt_all, shift, 1)
    da = pltpu.roll(da_all, shift, 1)

    row = lax.broadcasted_iota(jnp.int32, (Q, Q), 0)
    col = lax.broadcasted_iota(jnp.int32, (Q, Q), 1)
    causal = row >= col
    tri = causal.astype(BF16)
    hi, mid, lo = _split3(da)
    cs = (jnp.dot(tri, hi, preferred_element_type=F32)
          + jnp.dot(tri, mid, preferred_element_type=F32)
          + jnp.dot(tri, lo, preferred_element_type=F32))
    cs_t = cs.T
    dt_t = dt.T
    w_t = jnp.exp(cs_t[:, Q - 1:Q] - cs_t) * dt_t
    e_last = jnp.exp(cs[Q - 1:Q, :])

    cb = lax.dot_general(cg.astype(BF16), bg.astype(BF16), (((1,), (1,)), ((), ())),
                         preferred_element_type=F32)
    bg_t = bg.T

    m_parts, cs_parts, bw_parts = [], [], []
    for r in range(HEADS_PER_GROUP):
        colb = jnp.broadcast_to(cs[:, r:r + 1], (Q, LANES))
        seg = jnp.concatenate([colb] * (Q // LANES), axis=1) - cs_t[r:r + 1, :]
        decay = jnp.exp(jnp.where(causal, seg, -jnp.inf))
        m_parts.append((cb * decay * dt_t[r:r + 1, :]).astype(BF16))
        cs_parts.append((cg * jnp.exp(colb)).astype(BF16))
        bw_parts.append((bg_t * w_t[r:r + 1, :]).astype(BF16))

    lane_head = lax.shift_right_logical(
        lax.broadcasted_iota(jnp.int32, (1, HALF * SSD_HEAD_DIM), 1), 6)
    y_halves = []
    for k in range(HEADS_PER_GROUP // HALF):
        lo_c, hi_c = k * HALF * SSD_HEAD_DIM, (k + 1) * HALF * SSD_HEAD_DIM
        x_k = xs[:, lo_c:hi_c]
        s_k = st_sc[:, lo_c:hi_c]
        bd_x = jnp.concatenate(
            [jnp.where(lane_head == j, x_k, 0.0).astype(BF16) for j in range(HALF)], axis=0)
        bd_s = jnp.concatenate(
            [jnp.where(lane_head == j, s_k, 0.0).astype(BF16) for j in range(HALF)], axis=0)
        heads = range(k * HALF, (k + 1) * HALF)
        lhs = jnp.concatenate([m_parts[r] for r in heads] + [cs_parts[r] for r in heads], axis=1)
        rhs = jnp.concatenate([bd_x, bd_s], axis=0)
        y_halves.append(jnp.dot(lhs, rhs, preferred_element_type=F32))
        contrib = jnp.dot(jnp.concatenate([bw_parts[r] for r in heads], axis=1), bd_x,
                          preferred_element_type=F32)
        dec = jnp.zeros((1, HALF * SSD_HEAD_DIM), F32)
        for j, r in enumerate(heads):
            dec = jnp.where(lane_head == j, e_last[:, r:r + 1], dec)
        st_sc[:, lo_c:hi_c] = s_k * dec + contrib

    y = jnp.concatenate(y_halves, axis=1) + xs * dsk_ref[...]
    y = y * _silu(z_ref[0])
    y_ref[0] = _rms(y, nw_ref[...]).astype(BF16)


def _ssd_scan(zx, dt_raw, cw, cb, dt_bias, a_log, d_skip, norm_w, bsz, seq_len):
    nc = pl.cdiv(seq_len, Q)
    zx3 = zx.reshape(bsz, seq_len, ZX_DIM)
    dt3 = dt_raw.reshape(bsz, seq_len, LANES)
    x_blk = D_INNER // GROUP_DIM
    b_blk = 2 * D_INNER // D_STATE
    c_blk = b_blk + SSD_GROUPS
    cwx, cwb, cwc = cw[:, :D_INNER], cw[:, D_INNER:D_INNER + 1024], cw[:, D_INNER + 1024:]
    cbx, cbb, cbc = cb[:, :D_INNER], cb[:, D_INNER:D_INNER + 1024], cb[:, D_INNER + 1024:]
    grp = lambda w: pl.BlockSpec((w.shape[0], w.shape[1] // SSD_GROUPS), lambda b, g, c: (0, g))
    whole = lambda w: pl.BlockSpec(w.shape, lambda b, g, c: (0, 0))
    return pl.pallas_call(
        functools.partial(_scan_kernel, seq_len),
        grid=(bsz, SSD_GROUPS, nc),
        in_specs=[
            pl.BlockSpec((1, Q, GROUP_DIM), lambda b, g, c: (b, c, g)),
            pl.BlockSpec((1, Q, GROUP_DIM), lambda b, g, c: (b, c, x_blk + g)),
            pl.BlockSpec((1, Q, D_STATE), lambda b, g, c: (b, c, b_blk + g)),
            pl.BlockSpec((1, Q, D_STATE), lambda b, g, c: (b, c, c_blk + g)),
            pl.BlockSpec((1, Q, LANES), lambda b, g, c: (b, c, 0)),
            grp(cwx), grp(cwb), grp(cwc), grp(cbx), grp(cbb), grp(cbc),
            whole(dt_bias), whole(a_log), grp(d_skip), grp(norm_w),
        ],
        out_specs=pl.BlockSpec((1, Q, GROUP_DIM), lambda b, g, c: (b, c, g)),
        out_shape=jax.ShapeDtypeStruct((bsz, seq_len, D_INNER), BF16),
        scratch_shapes=[
            pltpu.VMEM((D_STATE, GROUP_DIM), F32),
            pltpu.VMEM((Q + SUBLANES, GROUP_DIM), F32),
            pltpu.VMEM((Q + SUBLANES, D_STATE), F32),
            pltpu.VMEM((Q + SUBLANES, D_STATE), F32),
        ],
        compiler_params=_params("arbitrary", "arbitrary", "arbitrary"),
        name="ssd_scan",
    )(zx3, zx3, zx3, zx3, dt3, cwx, cwb, cwc, cbx, cbb, cbc, dt_bias, a_log, d_skip, norm_w)


def _outproj_kernel(y_ref, w_ref, h_ref, nw_ref, o_ref, acc_sc):
    j = pl.program_id(1)
    acc_sc[j] = jnp.dot(y_ref[...], w_ref[...], preferred_element_type=F32)

    @pl.when(j == pl.num_programs(1) - 1)
    def _():
        nj = acc_sc.shape[0]
        ssq = sum(jnp.sum(acc_sc[k] * acc_sc[k], axis=-1, keepdims=True) for k in range(nj))
        inv = lax.rsqrt(ssq / D_MODEL + EPS)
        for k in range(nj):
            sl = slice(k * TN, (k + 1) * TN)
            o_ref[:, sl] = h_ref[:, sl] + acc_sc[k] * inv * nw_ref[:, sl]


def _outproj(y, w, h, nw):
    t = h.shape[0]
    nj = D_MODEL // TN
    return pl.pallas_call(
        _outproj_kernel,
        grid=(t // TM, nj),
        in_specs=[
            pl.BlockSpec((TM, D_INNER), lambda i, j: (i, 0)),
            pl.BlockSpec((D_INNER, TN), lambda i, j: (0, j)),
            pl.BlockSpec((TM, D_MODEL), lambda i, j: (i, 0)),
            pl.BlockSpec((1, D_MODEL), lambda i, j: (0, 0)),
        ],
        out_specs=pl.BlockSpec((TM, D_MODEL), lambda i, j: (i, 0)),
        out_shape=jax.ShapeDtypeStruct((t, D_MODEL), F32),
        scratch_shapes=[pltpu.VMEM((nj, TM, TN), F32)],
        compiler_params=_params("arbitrary", "arbitrary"),
        name="ssd_outproj",
    )(y, w, h, nw)


HALO = 16


def _pool_kernel(h_ref, halo_ref, nw0_ref, w_ref, b_ref, sc_ref, nw1_ref, o_ref):
    tile_in_row = pl.program_id(0) % 3
    h = h_ref[...]
    u = _rms(h, nw0_ref[...])
    uh = _rms(halo_ref[...], nw0_ref[...])
    uh = jnp.where(tile_in_row == 0, 0.0, uh)
    full = jnp.concatenate([uh, u], axis=0)
    pos = lax.broadcasted_iota(jnp.int32, (TM, 1), 0) + tile_in_row * TM
    outs = []
    for gi, win in enumerate(POOL_WINDOWS):
        sl = slice(gi * POOL_GROUP_DIM, (gi + 1) * POOL_GROUP_DIM)
        s = full[:, sl]
        k = 1
        while k < win:
            s = s + pltpu.roll(s, k, 0)
            k *= 2
        count = jnp.minimum(pos + 1, win).astype(F32)
        mixed = (s[HALO:] / count - u[:, sl]).astype(BF16)
        o = jnp.dot(mixed, w_ref[gi], preferred_element_type=F32) + b_ref[:, sl]
        outs.append(o * sc_ref[:, sl])
    mix = jnp.concatenate(outs, axis=1)
    o_ref[...] = h + _rms(mix, nw1_ref[...])


def _pool_layer(h, nw0, w, b, scale, nw1):
    t = h.shape[0]
    vec = pl.BlockSpec((1, D_MODEL), lambda i: (0, 0))
    return pl.pallas_call(
        _pool_kernel,
        grid=(t // TM,),
        in_specs=[
            pl.BlockSpec((TM, D_MODEL), lambda i: (i, 0)),
            pl.BlockSpec((HALO, D_MODEL), lambda i: (jnp.maximum(i * (TM // HALO) - 1, 0), 0)),
            vec,
            pl.BlockSpec(w.shape, lambda i: (0, 0, 0)),
            vec, vec, vec,
        ],
        out_specs=pl.BlockSpec((TM, D_MODEL), lambda i: (i, 0)),
        out_shape=jax.ShapeDtypeStruct((t, D_MODEL), F32),
        compiler_params=_params("arbitrary"),
        name="pool_layer",
    )(h, h, nw0, w, b, scale, nw1)


def _ffn_kernel(h_ref, nw2_ref, wg_ref, wu_ref, wd_ref, nw3_ref, o_ref, u_sc, acc_sc):
    j = pl.program_id(1)

    @pl.when(j == 0)
    def _():
        u_sc[...] = _rms(h_ref[...], nw2_ref[...]).astype(BF16)

    u = u_sc[...]
    gate = jnp.dot(u, wg_ref[...], preferred_element_type=F32)
    up = jnp.dot(u, wu_ref[...], preferred_element_type=F32)
    act = (_silu(gate) * up).astype(BF16)
    down = jnp.dot(act, wd_ref[...], preferred_element_type=F32)

    @pl.when(j == 0)
    def _():
        acc_sc[...] = down

    @pl.when(j > 0)
    def _():
        acc_sc[...] += down

    @pl.when(j == pl.num_programs(1) - 1)
    def _():
        o_ref[...] = h_ref[...] + _rms(acc_sc[...], nw3_ref[...])


def _ffn_layer(h, nw2, wg, wu, wd, nw3):
    t = h.shape[0]
    vec = pl.BlockSpec((1, D_MODEL), lambda i, j: (0, 0))
    return pl.pallas_call(
        _ffn_kernel,
        grid=(t // TM, FFN_HIDDEN // TN),
        in_specs=[
            pl.BlockSpec((TM, D_MODEL), lambda i, j: (i, 0)),
            vec,
            pl.BlockSpec((D_MODEL, TN), lambda i, j: (0, j)),
            pl.BlockSpec((D_MODEL, TN), lambda i, j: (0, j)),
            pl.BlockSpec((TN, D_MODEL), lambda i, j: (j, 0)),
            vec,
        ],
        out_specs=pl.BlockSpec((TM, D_MODEL), lambda i, j: (i, 0)),
        out_shape=jax.ShapeDtypeStruct((t, D_MODEL), F32),
        scratch_shapes=[pltpu.VMEM((TM, D_MODEL), BF16), pltpu.VMEM((TM, D_MODEL), F32)],
        compiler_params=_params("arbitrary", "arbitrary"),
        name="ffn_layer",
    )(h, nw2, wg, wu, wd, nw3)


def _pad_lanes(v):
    return jnp.pad(v, ((0, 0), (0, LANES - v.shape[1])))


def kernel(x, meta_tokens, norm_w, ssd_w_in, ssd_conv_w, ssd_conv_b, ssd_dt_bias, ssd_a_log, ssd_d,
           ssd_norm_w, ssd_w_out, pool_w, pool_b, pool_scale, ffn_w_gate, ffn_w_up, ffn_w_down):
    bsz, seq, _ = x.shape
    seq_len = N_META + seq
    assert seq_len % TM == 0 and TM % HALO == 0
    depth = norm_w.shape[0]
    meta = jnp.broadcast_to(meta_tokens[None].astype(x.dtype), (bsz, N_META, D_MODEL))
    h = jnp.concatenate([meta, x], axis=1).reshape(bsz * seq_len, D_MODEL)
    for i in range(depth):
        j = i // 2
        nw = norm_w[i][:, None, :]
        if i % 2 == 0:
            w_in = ssd_w_in[j]
            zx, dt_raw = _inproj(h, nw[0], w_in[:, :ZX_DIM].astype(BF16),
                                 _pad_lanes(w_in[:, ZX_DIM:]).astype(BF16))
            y = _ssd_scan(zx, dt_raw, ssd_conv_w[j], ssd_conv_b[j][None],
                          _pad_lanes(ssd_dt_bias[j][None]), _pad_lanes(ssd_a_log[j][None]),
                          jnp.repeat(ssd_d[j], SSD_HEAD_DIM)[None], ssd_norm_w[j][None],
                          bsz, seq_len)
            h = _outproj(y.reshape(bsz * seq_len, D_INNER), ssd_w_out[j].astype(BF16), h, nw[1])
        else:
            h = _pool_layer(h, nw[0], pool_w[j].astype(BF16), pool_b[j][None],
                            pool_scale[j][None], nw[1])
        h = _ffn_layer(h, nw[2], ffn_w_gate[i].astype(BF16), ffn_w_up[i].astype(BF16),
                       ffn_w_down[i].astype(BF16), nw[3])
    return h.reshape(bsz, seq_len, D_MODEL)[:, N_META:]
```

```python
import functools

import jax
import jax.numpy as jnp
from jax import lax
from jax.experimental import pallas as pl
from jax.experimental.pallas import tpu as pltpu

F32 = jnp.float32
BF16 = jnp.bfloat16

D_MODEL = 2048
N_META = 16
EPS = 1e-6
LOG2E = 1.4426950408889634

D_INNER = 4096
SSD_HEADS = 64
SSD_HEAD_DIM = 64
D_STATE = 128
SSD_GROUPS = 8
HEADS_PER_GROUP = 8
GROUP_DIM = D_INNER // SSD_GROUPS
BC_DIM = SSD_GROUPS * D_STATE
D_CONV = 4
ZX_DIM = 2 * D_INNER + 2 * BC_DIM

POOL_WINDOWS = (2, 4, 8, 16)
POOL_GROUP_DIM = 512
FFN_HIDDEN = 5632

LANES = 128
SUBLANES = 8
TM = 688
TILES_PER_ROW = 3
TN = 512
TN_IN = 1024
Z_TILES = D_INNER // TN_IN
Q = 256
HALF = 4
VMEM_LIMIT = 56 * 1024 * 1024


def _rms(x, w):
    return x * lax.rsqrt(jnp.mean(x * x, axis=-1, keepdims=True) + EPS) * w


def _silu(x):
    hx = 0.5 * x
    return hx + hx * jnp.tanh(hx)


def _params(*sem):
    return pltpu.CompilerParams(dimension_semantics=sem, vmem_limit_bytes=VMEM_LIMIT)


def _inproj_kernel(h_ref, nw_ref, w_ref, wdt_ref, cw_ref, cb_ref, o_ref, odt_ref,
                   u_sc, raw_sc, halo_sc):
    i = pl.program_id(0)
    j = pl.program_id(1)

    @pl.when(j == 0)
    def _():
        u = _rms(h_ref[...], nw_ref[...]).astype(BF16)
        u_sc[...] = u
        odt_ref[...] = jnp.dot(u, wdt_ref[...], preferred_element_type=F32)

    @pl.when(j < Z_TILES)
    def _():
        o_ref[...] = _silu(jnp.dot(u_sc[...], w_ref[...], preferred_element_type=F32))

    @pl.when(j >= Z_TILES)
    def _():
        slot = j - Z_TILES
        first = i % TILES_PER_ROW == 0

        @pl.when(first)
        def _():
            raw_sc[0:SUBLANES, :] = jnp.zeros((SUBLANES, TN_IN), F32)

        @pl.when(jnp.logical_not(first))
        def _():
            raw_sc[0:SUBLANES, :] = halo_sc[slot]

        raw_sc[SUBLANES:, :] = jnp.dot(u_sc[...], w_ref[...], preferred_element_type=F32)
        acc = cb_ref[...]
        for k in range(D_CONV):
            off = SUBLANES - (D_CONV - 1) + k
            acc = acc + cw_ref[k:k + 1, :] * raw_sc[off:off + TM, :]
        halo_sc[slot] = raw_sc[TM:TM + SUBLANES, :]
        o_ref[...] = _silu(acc)


def _inproj(h, nw, w, wdt, cw, cb):
    t = h.shape[0]
    nj = ZX_DIM // TN_IN
    conv_tile = lambda i, j: (0, jnp.maximum(j - Z_TILES, 0))
    return pl.pallas_call(
        _inproj_kernel,
        grid=(t // TM, nj),
        in_specs=[
            pl.BlockSpec((TM, D_MODEL), lambda i, j: (i, 0)),
            pl.BlockSpec((1, D_MODEL), lambda i, j: (0, 0)),
            pl.BlockSpec((D_MODEL, TN_IN), lambda i, j: (0, j)),
            pl.BlockSpec((D_MODEL, LANES), lambda i, j: (0, 0)),
            pl.BlockSpec((D_CONV, TN_IN), conv_tile),
            pl.BlockSpec((1, TN_IN), conv_tile),
        ],
        out_specs=[
            pl.BlockSpec((TM, TN_IN), lambda i, j: (i, j)),
            pl.BlockSpec((TM, LANES), lambda i, j: (i, 0)),
        ],
        out_shape=[
            jax.ShapeDtypeStruct((t, ZX_DIM), F32),
            jax.ShapeDtypeStruct((t, LANES), F32),
        ],
        scratch_shapes=[
            pltpu.VMEM((TM, D_MODEL), BF16),
            pltpu.VMEM((TM + SUBLANES, TN_IN), F32),
            pltpu.VMEM((nj - Z_TILES, SUBLANES, TN_IN), F32),
        ],
        compiler_params=_params("arbitrary", "arbitrary"),
        name="ssd_inproj",
    )(h, nw, w, wdt, cw, cb)


def _split3(v):
    hi = v.astype(BF16)
    r1 = v - hi.astype(F32)
    mid = r1.astype(BF16)
    lo = (r1 - mid.astype(F32)).astype(BF16)
    return hi, mid, lo


def _scan_kernel(seq_len, gz_ref, x_ref, b_ref, c_ref, dt_ref, dtb_ref, alog_ref, dsk_ref, nw_ref,
                 y_ref, st_sc):
    c = pl.program_id(1)

    @pl.when(c == 0)
    def _():
        st_sc[...] = jnp.zeros_like(st_sc)

    valid = lax.broadcasted_iota(jnp.int32, (Q, 1), 0) < (seq_len - c * Q)

    dt = jnp.where(valid, jax.nn.softplus(dt_ref[0] + dtb_ref[...]), 0.0)
    da = dt * (-jnp.exp(alog_ref[...]))

    row = lax.broadcasted_iota(jnp.int32, (Q, Q), 0)
    col = lax.broadcasted_iota(jnp.int32, (Q, Q), 1)
    causal = row >= col
    tri = causal.astype(BF16)
    hi, mid, lo = _split3(da)
    cs = (jnp.dot(tri, hi, preferred_element_type=F32)
          + jnp.dot(tri, mid, preferred_element_type=F32)
          + jnp.dot(tri, lo, preferred_element_type=F32))
    cs2 = cs * LOG2E
    cs2_t = cs2.T
    src_t = cs2_t - jnp.log2(dt.T)
    w_t = jnp.exp2(cs2_t[:, Q - 1:Q] - src_t)
    e_last = jnp.exp2(cs2[Q - 1:Q, :])

    lane_head = lax.shift_right_logical(
        lax.broadcasted_iota(jnp.int32, (1, HALF * SSD_HEAD_DIM), 1), 6)
    head_mask = [(lane_head == j).astype(BF16) for j in range(HALF)]

    for g in range(SSD_GROUPS):
        ch = slice(g * GROUP_DIM, (g + 1) * GROUP_DIM)
        st = slice(g * D_STATE, (g + 1) * D_STATE)
        xs = jnp.where(valid, x_ref[0, :, ch], 0.0)
        bg = jnp.where(valid, b_ref[0, :, st], 0.0)
        cg = jnp.where(valid, c_ref[0, :, st], 0.0)
        cb = lax.dot_general(cg.astype(BF16), bg.astype(BF16), (((1,), (1,)), ((), ())),
                             preferred_element_type=F32)
        bg_t = bg.T

        m_parts, cs_parts, bw_parts = [], [], []
        for r in range(HEADS_PER_GROUP):
            hd = g * HEADS_PER_GROUP + r
            colb = jnp.broadcast_to(cs2[:, hd:hd + 1], (Q, LANES))
            seg = jnp.concatenate([colb] * (Q // LANES), axis=1) - src_t[hd:hd + 1, :]
            decay = jnp.exp2(jnp.where(causal, seg, -jnp.inf))
            m_parts.append((cb * decay).astype(BF16))
            cs_parts.append((cg * jnp.exp2(colb)).astype(BF16))
            bw_parts.append((bg_t * w_t[hd:hd + 1, :]).astype(BF16))

        y_halves = []
        for k in range(HEADS_PER_GROUP // HALF):
            lo_c, hi_c = k * HALF * SSD_HEAD_DIM, (k + 1) * HALF * SSD_HEAD_DIM
            x_k = xs[:, lo_c:hi_c].astype(BF16)
            s_k = st_sc[g, :, lo_c:hi_c]
            s_kb = s_k.astype(BF16)
            bd_x = jnp.concatenate([x_k * head_mask[j] for j in range(HALF)], axis=0)
            bd_s = jnp.concatenate([s_kb * head_mask[j] for j in range(HALF)], axis=0)
            heads = range(k * HALF, (k + 1) * HALF)
            lhs = jnp.concatenate([m_parts[r] for r in heads] + [cs_parts[r] for r in heads], axis=1)
            rhs = jnp.concatenate([bd_x, bd_s], axis=0)
            y_halves.append(jnp.dot(lhs, rhs, preferred_element_type=F32))
            contrib = jnp.dot(jnp.concatenate([bw_parts[r] for r in heads], axis=1), bd_x,
                              preferred_element_type=F32)
            dec = jnp.zeros((1, HALF * SSD_HEAD_DIM), F32)
            for j, r in enumerate(heads):
                hd = g * HEADS_PER_GROUP + r
                dec = jnp.where(lane_head == j, e_last[:, hd:hd + 1], dec)
            st_sc[g, :, lo_c:hi_c] = s_k * dec + contrib

        y = jnp.concatenate(y_halves, axis=1) + xs * dsk_ref[:, ch]
        y = y * gz_ref[0, :, ch]
        y_ref[0, :, ch] = _rms(y, nw_ref[:, ch]).astype(BF16)


def _ssd_scan(zx, dt_raw, dt_bias, a_log, d_skip, norm_w, bsz, seq_len):
    nc = pl.cdiv(seq_len, Q)
    zx3 = zx.reshape(bsz, seq_len, ZX_DIM)
    dt3 = dt_raw.reshape(bsz, seq_len, LANES)
    b_blk = 2 * D_INNER // BC_DIM
    whole = lambda w: pl.BlockSpec(w.shape, lambda b, c: (0, 0))
    return pl.pallas_call(
        functools.partial(_scan_kernel, seq_len),
        grid=(bsz, nc),
        in_specs=[
            pl.BlockSpec((1, Q, D_INNER), lambda b, c: (b, c, 0)),
            pl.BlockSpec((1, Q, D_INNER), lambda b, c: (b, c, 1)),
            pl.BlockSpec((1, Q, BC_DIM), lambda b, c: (b, c, b_blk)),
            pl.BlockSpec((1, Q, BC_DIM), lambda b, c: (b, c, b_blk + 1)),
            pl.BlockSpec((1, Q, LANES), lambda b, c: (b, c, 0)),
            whole(dt_bias), whole(a_log), whole(d_skip), whole(norm_w),
        ],
        out_specs=pl.BlockSpec((1, Q, D_INNER), lambda b, c: (b, c, 0)),
        out_shape=jax.ShapeDtypeStruct((bsz, seq_len, D_INNER), BF16),
        scratch_shapes=[pltpu.VMEM((SSD_GROUPS, D_STATE, GROUP_DIM), F32)],
        compiler_params=_params("arbitrary", "arbitrary"),
        name="ssd_scan",
    )(zx3, zx3, zx3, zx3, dt3, dt_bias, a_log, d_skip, norm_w)


def _outproj_kernel(y_ref, w_ref, h_ref, nw_ref, o_ref, acc_sc):
    j = pl.program_id(1)
    acc_sc[j] = jnp.dot(y_ref[...], w_ref[...], preferred_element_type=F32)

    @pl.when(j == pl.num_programs(1) - 1)
    def _():
        nj = acc_sc.shape[0]
        ssq = sum(jnp.sum(acc_sc[k] * acc_sc[k], axis=-1, keepdims=True) for k in range(nj))
        inv = lax.rsqrt(ssq / D_MODEL + EPS)
        for k in range(nj):
            sl = slice(k * TN, (k + 1) * TN)
            o_ref[:, sl] = h_ref[:, sl] + acc_sc[k] * inv * nw_ref[:, sl]


def _outproj(y, w, h, nw):
    t = h.shape[0]
    nj = D_MODEL // TN
    return pl.pallas_call(
        _outproj_kernel,
        grid=(t // TM, nj),
        in_specs=[
            pl.BlockSpec((TM, D_INNER), lambda i, j: (i, 0)),
            pl.BlockSpec((D_INNER, TN), lambda i, j: (0, j)),
            pl.BlockSpec((TM, D_MODEL), lambda i, j: (i, 0)),
            pl.BlockSpec((1, D_MODEL), lambda i, j: (0, 0)),
        ],
        out_specs=pl.BlockSpec((TM, D_MODEL), lambda i, j: (i, 0)),
        out_shape=jax.ShapeDtypeStruct((t, D_MODEL), F32),
        scratch_shapes=[pltpu.VMEM((nj, TM, TN), F32)],
        compiler_params=_params("arbitrary", "arbitrary"),
        name="ssd_outproj",
    )(y, w, h, nw)


HALO = 16


def _pool_kernel(h_ref, halo_ref, nw0_ref, w_ref, b_ref, sc_ref, nw1_ref, o_ref):
    tile_in_row = pl.program_id(0) % TILES_PER_ROW
    h = h_ref[...]
    u = _rms(h, nw0_ref[...])
    uh = _rms(halo_ref[...], nw0_ref[...])
    uh = jnp.where(tile_in_row == 0, 0.0, uh)
    full = jnp.concatenate([uh, u], axis=0)
    pos = lax.broadcasted_iota(jnp.int32, (TM, 1), 0) + tile_in_row * TM
    outs = []
    for gi, win in enumerate(POOL_WINDOWS):
        sl = slice(gi * POOL_GROUP_DIM, (gi + 1) * POOL_GROUP_DIM)
        s = full[:, sl]
        k = 1
        while k < win:
            s = s + pltpu.roll(s, k, 0)
            k *= 2
        count = jnp.minimum(pos + 1, win).astype(F32)
        mixed = (s[HALO:] / count - u[:, sl]).astype(BF16)
        o = jnp.dot(mixed, w_ref[gi], preferred_element_type=F32) + b_ref[:, sl]
        outs.append(o * sc_ref[:, sl])
    mix = jnp.concatenate(outs, axis=1)
    o_ref[...] = h + _rms(mix, nw1_ref[...])


def _pool_layer(h, nw0, w, b, scale, nw1):
    t = h.shape[0]
    vec = pl.BlockSpec((1, D_MODEL), lambda i: (0, 0))
    return pl.pallas_call(
        _pool_kernel,
        grid=(t // TM,),
        in_specs=[
            pl.BlockSpec((TM, D_MODEL), lambda i: (i, 0)),
            pl.BlockSpec((HALO, D_MODEL), lambda i: (jnp.maximum(i * (TM // HALO) - 1, 0), 0)),
            vec,
            pl.BlockSpec(w.shape, lambda i: (0, 0, 0)),
            vec, vec, vec,
        ],
        out_specs=pl.BlockSpec((TM, D_MODEL), lambda i: (i, 0)),
        out_shape=jax.ShapeDtypeStruct((t, D_MODEL), F32),
        compiler_params=_params("arbitrary"),
        name="pool_layer",
    )(h, h, nw0, w, b, scale, nw1)


def _ffn_kernel(h_ref, nw2_ref, wg_ref, wu_ref, wd_ref, nw3_ref, o_ref, u_sc, acc_sc):
    j = pl.program_id(1)

    @pl.when(j == 0)
    def _():
        u_sc[...] = _rms(h_ref[...], nw2_ref[...]).astype(BF16)
        acc_sc[...] = jnp.zeros_like(acc_sc)

    u = u_sc[...]
    gate = jnp.dot(u, wg_ref[...], preferred_element_type=F32)
    up = jnp.dot(u, wu_ref[...], preferred_element_type=F32)
    act = (_silu(gate) * up).astype(BF16)
    acc_sc[...] += jnp.dot(act, wd_ref[...], preferred_element_type=F32)

    @pl.when(j == pl.num_programs(1) - 1)
    def _():
        o_ref[...] = h_ref[...] + _rms(acc_sc[...], nw3_ref[...])


def _ffn_layer(h, nw2, wg, wu, wd, nw3):
    t = h.shape[0]
    vec = pl.BlockSpec((1, D_MODEL), lambda i, j: (0, 0))
    return pl.pallas_call(
        _ffn_kernel,
        grid=(t // TM, FFN_HIDDEN // TN),
        in_specs=[
            pl.BlockSpec((TM, D_MODEL), lambda i, j: (i, 0)),
            vec,
            pl.BlockSpec((D_MODEL, TN), lambda i, j: (0, j)),
            pl.BlockSpec((D_MODEL, TN), lambda i, j: (0, j)),
            pl.BlockSpec((TN, D_MODEL), lambda i, j: (j, 0)),
            vec,
        ],
        out_specs=pl.BlockSpec((TM, D_MODEL), lambda i, j: (i, 0)),
        out_shape=jax.ShapeDtypeStruct((t, D_MODEL), F32),
        scratch_shapes=[pltpu.VMEM((TM, D_MODEL), BF16), pltpu.VMEM((TM, D_MODEL), F32)],
        compiler_params=_params("arbitrary", "arbitrary"),
        name="ffn_layer",
    )(h, nw2, wg, wu, wd, nw3)


def _pad_lanes(v):
    return jnp.pad(v, ((0, 0), (0, LANES - v.shape[1])))


def kernel(x, meta_tokens, norm_w, ssd_w_in, ssd_conv_w, ssd_conv_b, ssd_dt_bias, ssd_a_log, ssd_d,
           ssd_norm_w, ssd_w_out, pool_w, pool_b, pool_scale, ffn_w_gate, ffn_w_up, ffn_w_down):
    bsz, seq, _ = x.shape
    seq_len = N_META + seq
    assert seq_len == TILES_PER_ROW * TM and TM % HALO == 0
    depth = norm_w.shape[0]
    meta = jnp.broadcast_to(meta_tokens[None].astype(x.dtype), (bsz, N_META, D_MODEL))
    h = jnp.concatenate([meta, x], axis=1).reshape(bsz * seq_len, D_MODEL)
    for i in range(depth):
        j = i // 2
        nw = norm_w[i][:, None, :]
        if i % 2 == 0:
            w_in = ssd_w_in[j]
            zx, dt_raw = _inproj(h, nw[0], w_in[:, :ZX_DIM].astype(BF16),
                                 _pad_lanes(w_in[:, ZX_DIM:]).astype(BF16),
                                 ssd_conv_w[j], ssd_conv_b[j][None])
            y = _ssd_scan(zx, dt_raw, _pad_lanes(ssd_dt_bias[j][None]),
                          _pad_lanes(ssd_a_log[j][None]),
                          jnp.repeat(ssd_d[j], SSD_HEAD_DIM)[None], ssd_norm_w[j][None],
                          bsz, seq_len)
            h = _outproj(y.reshape(bsz * seq_len, D_INNER), ssd_w_out[j].astype(BF16), h, nw[1])
        else:
            h = _pool_layer(h, nw[0], pool_w[j].astype(BF16), pool_b[j][None],
                            pool_scale[j][None], nw[1])
        h = _ffn_layer(h, nw[2], ffn_w_gate[i].astype(BF16), ffn_w_up[i].astype(BF16),
                       ffn_w_down[i].astype(BF16), nw[3])
    return h.reshape(bsz, seq_len, D_MODEL)[:, N_META:]
```

```python
import functools

import jax
import jax.numpy as jnp
from jax import lax
from jax.experimental import pallas as pl
from jax.experimental.pallas import tpu as pltpu

F32 = jnp.float32
BF16 = jnp.bfloat16

D_MODEL = 2048
N_META = 16
EPS = 1e-6
LOG2E = 1.4426950408889634

D_INNER = 4096
SSD_HEADS = 64
SSD_HEAD_DIM = 64
D_STATE = 128
SSD_GROUPS = 8
HEADS_PER_GROUP = 8
GROUP_DIM = D_INNER // SSD_GROUPS
BC_DIM = SSD_GROUPS * D_STATE
D_CONV = 4
ZX_DIM = 2 * D_INNER + 2 * BC_DIM

POOL_WINDOWS = (2, 4, 8, 16)
POOL_GROUP_DIM = 512
FFN_HIDDEN = 5632

LANES = 128
SUBLANES = 8
TM = 688
TILES_PER_ROW = 3
TN = 512
TN_IN = 1024
Z_TILES = D_INNER // TN_IN
Q = 256
HALF = 4
VMEM_LIMIT = 56 * 1024 * 1024


def _rms(x, w):
    return x * lax.rsqrt(jnp.mean(x * x, axis=-1, keepdims=True) + EPS) * w


def _silu(x):
    hx = 0.5 * x
    return hx + hx * jnp.tanh(hx)


def _dot_nt(a, b_t):
    return lax.dot_general(a, b_t, (((1,), (1,)), ((), ())), preferred_element_type=F32)


def _params(*sem):
    return pltpu.CompilerParams(dimension_semantics=sem, vmem_limit_bytes=VMEM_LIMIT)


def _cast_specs(casts, step_of):
    in_specs, out_specs, out_shapes = [], [], []
    for w, layer, rows, chunks in casts:
        assert rows % (chunks * 16) == 0
        blk, cols = rows // chunks, w.shape[2]
        chunk_of = lambda *ids, n=chunks: jnp.minimum(step_of(*ids), n - 1)
        in_specs.append(pl.BlockSpec((None, blk, cols),
                                     lambda *ids, l=layer, f=chunk_of: (l, f(*ids), 0)))
        out_specs.append(pl.BlockSpec((blk, cols), lambda *ids, f=chunk_of: (f(*ids), 0)))
        out_shapes.append(jax.ShapeDtypeStruct((rows, cols), BF16))
    return in_specs, out_specs, out_shapes


def _cast_alone(cast):
    in_specs, out_specs, out_shapes = _cast_specs([cast], lambda s: s)
    return pl.pallas_call(
        _with_casts(lambda: None, 0, 0, 1),
        grid=(cast[3],),
        in_specs=in_specs, out_specs=out_specs, out_shape=out_shapes,
        compiler_params=_params("arbitrary"),
        name="weight_cast",
    )(cast[0])[0]


def _with_casts(body, n_in, n_out, n_cast):
    def kernel_fn(*refs):
        ins, refs = refs[:n_in], refs[n_in:]
        cast_in, refs = refs[:n_cast], refs[n_cast:]
        outs, refs = refs[:n_out], refs[n_out:]
        cast_out, scratch = refs[:n_cast], refs[n_cast:]
        for src, dst in zip(cast_in, cast_out):
            dst[...] = src[...].astype(BF16)
        body(*ins, *outs, *scratch)
    return kernel_fn


def _inproj_kernel(h_ref, nw_ref, w_ref, wdt_ref, cw_ref, cb_ref, o_ref, odt_ref, u_sc, halo_sc):
    i = pl.program_id(0)
    j = pl.program_id(1)

    @pl.when(j == 0)
    def _():
        u = _rms(h_ref[...], nw_ref[...]).astype(BF16)
        u_sc[...] = u
        wdt = wdt_ref[...].astype(BF16)
        wdt = jnp.concatenate([wdt, jnp.zeros_like(wdt)], axis=0)
        odt_ref[...] = _dot_nt(u, wdt)

    @pl.when(jnp.logical_and(i == 0, j == 0))
    def _():
        halo_sc[...] = jnp.zeros_like(halo_sc)

    @pl.when(j < Z_TILES)
    def _():
        o_ref[...] = _silu(_dot_nt(u_sc[...], w_ref[...]))

    @pl.when(j >= Z_TILES)
    def _():
        slot = j - Z_TILES
        raw = _dot_nt(u_sc[...], w_ref[...])
        halo = jnp.where(i % TILES_PER_ROW == 0, 0.0, halo_sc[slot])
        full = jnp.concatenate([halo, raw], axis=0)
        acc = cb_ref[...] + cw_ref[D_CONV - 1:D_CONV, :] * raw
        for back in range(1, D_CONV):
            w_tap = cw_ref[D_CONV - 1 - back:D_CONV - back, :]
            acc = acc + w_tap * pltpu.roll(full, back, 0)[SUBLANES:]
        halo_sc[slot] = raw[TM - SUBLANES:]
        o_ref[...] = _silu(acc)


def _inproj(h, nw, w_t, w_in_t_f32, layer, cw, cb):
    t = h.shape[0]
    nj = ZX_DIM // TN_IN
    conv_tile = lambda i, j: (0, jnp.maximum(j - Z_TILES, 0))
    return pl.pallas_call(
        _inproj_kernel,
        grid=(t // TM, nj),
        in_specs=[
            pl.BlockSpec((TM, D_MODEL), lambda i, j: (i, 0)),
            pl.BlockSpec((1, D_MODEL), lambda i, j: (0, 0)),
            pl.BlockSpec((TN_IN, D_MODEL), lambda i, j: (j, 0)),
            pl.BlockSpec((None, SSD_HEADS, D_MODEL), lambda i, j: (layer, ZX_DIM // SSD_HEADS, 0)),
            pl.BlockSpec((D_CONV, TN_IN), conv_tile),
            pl.BlockSpec((1, TN_IN), conv_tile),
        ],
        out_specs=[
            pl.BlockSpec((TM, TN_IN), lambda i, j: (i, j)),
            pl.BlockSpec((TM, LANES), lambda i, j: (i, 0)),
        ],
        out_shape=[
            jax.ShapeDtypeStruct((t, ZX_DIM), F32),
            jax.ShapeDtypeStruct((t, LANES), F32),
        ],
        scratch_shapes=[
            pltpu.VMEM((TM, D_MODEL), BF16),
            pltpu.VMEM((nj - Z_TILES, SUBLANES, TN_IN), F32),
        ],
        compiler_params=_params("arbitrary", "arbitrary"),
        name="ssd_inproj",
    )(h, nw, w_t, w_in_t_f32, cw, cb)


def _split3(v):
    hi = v.astype(BF16)
    r1 = v - hi.astype(F32)
    mid = r1.astype(BF16)
    lo = (r1 - mid.astype(F32)).astype(BF16)
    return hi, mid, lo


def _scan_kernel(seq_len, gz_ref, x_ref, b_ref, c_ref, dt_ref, dtb_ref, alog_ref, dsk_ref, nw_ref,
                 y_ref, st_sc):
    c = pl.program_id(1)

    @pl.when(c == 0)
    def _():
        st_sc[...] = jnp.zeros_like(st_sc)

    valid = lax.broadcasted_iota(jnp.int32, (Q, 1), 0) < (seq_len - c * Q)

    dt = jnp.where(valid, jax.nn.softplus(dt_ref[0] + dtb_ref[...]), 0.0)
    da = dt * (-jnp.exp(alog_ref[...]))

    row = lax.broadcasted_iota(jnp.int32, (Q, Q), 0)
    col = lax.broadcasted_iota(jnp.int32, (Q, Q), 1)
    causal = row >= col
    tri = causal.astype(BF16)
    hi, mid, lo = _split3(da)
    cs = (jnp.dot(tri, hi, preferred_element_type=F32)
          + jnp.dot(tri, mid, preferred_element_type=F32)
          + jnp.dot(tri, lo, preferred_element_type=F32))
    cs2 = cs * LOG2E
    cs2_t = cs2.T
    src_t = cs2_t - jnp.log2(dt.T)
    w_t = jnp.exp2(cs2_t[:, Q - 1:Q] - src_t)
    e_last = jnp.exp2(cs2[Q - 1:Q, :])

    lane_head = lax.shift_right_logical(
        lax.broadcasted_iota(jnp.int32, (1, HALF * SSD_HEAD_DIM), 1), 6)
    head_mask = [(lane_head == j).astype(BF16) for j in range(HALF)]

    for g in range(SSD_GROUPS):
        ch = slice(g * GROUP_DIM, (g + 1) * GROUP_DIM)
        st = slice(g * D_STATE, (g + 1) * D_STATE)
        xs = jnp.where(valid, x_ref[0, :, ch], 0.0)
        bg = jnp.where(valid, b_ref[0, :, st], 0.0)
        cg = jnp.where(valid, c_ref[0, :, st], 0.0)
        cb = lax.dot_general(cg.astype(BF16), bg.astype(BF16), (((1,), (1,)), ((), ())),
                             preferred_element_type=F32)
        bg_t = bg.T

        m_parts, cs_parts, bw_parts = [], [], []
        for r in range(HEADS_PER_GROUP):
            hd = g * HEADS_PER_GROUP + r
            colb = jnp.broadcast_to(cs2[:, hd:hd + 1], (Q, LANES))
            seg = jnp.concatenate([colb] * (Q // LANES), axis=1) - src_t[hd:hd + 1, :]
            decay = jnp.exp2(jnp.where(causal, seg, -jnp.inf))
            m_parts.append((cb * decay).astype(BF16))
            cs_parts.append((cg * jnp.exp2(colb)).astype(BF16))
            bw_parts.append((bg_t * w_t[hd:hd + 1, :]).astype(BF16))

        y_halves = []
        for k in range(HEADS_PER_GROUP // HALF):
            lo_c, hi_c = k * HALF * SSD_HEAD_DIM, (k + 1) * HALF * SSD_HEAD_DIM
            x_k = xs[:, lo_c:hi_c].astype(BF16)
            s_k = st_sc[g, :, lo_c:hi_c]
            s_kb = s_k.astype(BF16)
            bd_x = jnp.concatenate([x_k * head_mask[j] for j in range(HALF)], axis=0)
            bd_s = jnp.concatenate([s_kb * head_mask[j] for j in range(HALF)], axis=0)
            heads = range(k * HALF, (k + 1) * HALF)
            lhs = jnp.concatenate([m_parts[r] for r in heads] + [cs_parts[r] for r in heads], axis=1)
            rhs = jnp.concatenate([bd_x, bd_s], axis=0)
            y_halves.append(jnp.dot(lhs, rhs, preferred_element_type=F32))
            contrib = jnp.dot(jnp.concatenate([bw_parts[r] for r in heads], axis=1), bd_x,
                              preferred_element_type=F32)
            dec = jnp.zeros((1, HALF * SSD_HEAD_DIM), F32)
            for j, r in enumerate(heads):
                hd = g * HEADS_PER_GROUP + r
                dec = jnp.where(lane_head == j, e_last[:, hd:hd + 1], dec)
            st_sc[g, :, lo_c:hi_c] = s_k * dec + contrib

        y = jnp.concatenate(y_halves, axis=1) + xs * dsk_ref[:, ch]
        y = y * gz_ref[0, :, ch]
        y_ref[0, :, ch] = _rms(y, nw_ref[:, ch]).astype(BF16)


def _ssd_scan(zx, dt_raw, dt_bias, a_log, d_skip, norm_w, bsz, seq_len, casts=()):
    nc = pl.cdiv(seq_len, Q)
    zx3 = zx.reshape(bsz, seq_len, ZX_DIM)
    dt3 = dt_raw.reshape(bsz, seq_len, LANES)
    b_blk = 2 * D_INNER // BC_DIM
    whole = lambda w: pl.BlockSpec(w.shape, lambda b, c: (0, 0))
    cast_in, cast_out, cast_shapes = _cast_specs(casts, lambda b, c: b * nc + c)
    in_specs = [
        pl.BlockSpec((1, Q, D_INNER), lambda b, c: (b, c, 0)),
        pl.BlockSpec((1, Q, D_INNER), lambda b, c: (b, c, 1)),
        pl.BlockSpec((1, Q, BC_DIM), lambda b, c: (b, c, b_blk)),
        pl.BlockSpec((1, Q, BC_DIM), lambda b, c: (b, c, b_blk + 1)),
        pl.BlockSpec((1, Q, LANES), lambda b, c: (b, c, 0)),
        whole(dt_bias), whole(a_log), whole(d_skip), whole(norm_w),
    ]
    return pl.pallas_call(
        _with_casts(functools.partial(_scan_kernel, seq_len), len(in_specs), 1, len(casts)),
        grid=(bsz, nc),
        in_specs=in_specs + cast_in,
        out_specs=[pl.BlockSpec((1, Q, D_INNER), lambda b, c: (b, c, 0))] + cast_out,
        out_shape=[jax.ShapeDtypeStruct((bsz, seq_len, D_INNER), BF16)] + cast_shapes,
        scratch_shapes=[pltpu.VMEM((SSD_GROUPS, D_STATE, GROUP_DIM), F32)],
        compiler_params=_params("arbitrary", "arbitrary"),
        name="ssd_scan",
    )(zx3, zx3, zx3, zx3, dt3, dt_bias, a_log, d_skip, norm_w, *[w for w, _, _, _ in casts])


def _outproj_kernel(y_ref, w_ref, h_ref, nw_ref, o_ref, acc_sc):
    j = pl.program_id(1)
    acc_sc[j] = jnp.dot(y_ref[...], w_ref[...], preferred_element_type=F32)

    @pl.when(j == pl.num_programs(1) - 1)
    def _():
        nj = acc_sc.shape[0]
        ssq = sum(jnp.sum(acc_sc[k] * acc_sc[k], axis=-1, keepdims=True) for k in range(nj))
        inv = lax.rsqrt(ssq / D_MODEL + EPS)
        for k in range(nj):
            sl = slice(k * TN, (k + 1) * TN)
            o_ref[:, sl] = h_ref[:, sl] + acc_sc[k] * inv * nw_ref[:, sl]


def _outproj(y, w, h, nw):
    t = h.shape[0]
    nj = D_MODEL // TN
    return pl.pallas_call(
        _outproj_kernel,
        grid=(t // TM, nj),
        in_specs=[
            pl.BlockSpec((TM, D_INNER), lambda i, j: (i, 0)),
            pl.BlockSpec((D_INNER, TN), lambda i, j: (0, j)),
            pl.BlockSpec((TM, D_MODEL), lambda i, j: (i, 0)),
            pl.BlockSpec((1, D_MODEL), lambda i, j: (0, 0)),
        ],
        out_specs=pl.BlockSpec((TM, D_MODEL), lambda i, j: (i, 0)),
        out_shape=jax.ShapeDtypeStruct((t, D_MODEL), F32),
        scratch_shapes=[pltpu.VMEM((nj, TM, TN), F32)],
        compiler_params=_params("arbitrary", "arbitrary"),
        name="ssd_outproj",
    )(y, w, h, nw)


HALO = 16


def _pool_kernel(h_ref, halo_ref, nw0_ref, w_ref, b_ref, sc_ref, nw1_ref, o_ref):
    tile_in_row = pl.program_id(0) % TILES_PER_ROW
    h = h_ref[...]
    u = _rms(h, nw0_ref[...])
    uh = _rms(halo_ref[...], nw0_ref[...])
    uh = jnp.where(tile_in_row == 0, 0.0, uh)
    full = jnp.concatenate([uh, u], axis=0)
    pos = lax.broadcasted_iota(jnp.int32, (TM, 1), 0) + tile_in_row * TM
    outs = []
    for gi, win in enumerate(POOL_WINDOWS):
        sl = slice(gi * POOL_GROUP_DIM, (gi + 1) * POOL_GROUP_DIM)
        s = full[:, sl]
        k = 1
        while k < win:
            s = s + pltpu.roll(s, k, 0)
            k *= 2
        count = jnp.minimum(pos + 1, win).astype(F32)
        mixed = (s[HALO:] / count - u[:, sl]).astype(BF16)
        o = jnp.dot(mixed, w_ref[gi], preferred_element_type=F32) + b_ref[:, sl]
        outs.append(o * sc_ref[:, sl])
    mix = jnp.concatenate(outs, axis=1)
    o_ref[...] = h + _rms(mix, nw1_ref[...])


def _pool_layer(h, nw0, w, b, scale, nw1):
    t = h.shape[0]
    vec = pl.BlockSpec((1, D_MODEL), lambda i: (0, 0))
    return pl.pallas_call(
        _pool_kernel,
        grid=(t // TM,),
        in_specs=[
            pl.BlockSpec((TM, D_MODEL), lambda i: (i, 0)),
            pl.BlockSpec((HALO, D_MODEL), lambda i: (jnp.maximum(i * (TM // HALO) - 1, 0), 0)),
            vec,
            pl.BlockSpec(w.shape, lambda i: (0, 0, 0)),
            vec, vec, vec,
        ],
        out_specs=pl.BlockSpec((TM, D_MODEL), lambda i: (i, 0)),
        out_shape=jax.ShapeDtypeStruct((t, D_MODEL), F32),
        compiler_params=_params("arbitrary"),
        name="pool_layer",
    )(h, h, nw0, w, b, scale, nw1)


def _ffn_kernel(h_ref, nw2_ref, wg_ref, wu_ref, wd_ref, nw3_ref, o_ref, u_sc, acc_sc):
    j = pl.program_id(1)

    @pl.when(j == 0)
    def _():
        u_sc[...] = _rms(h_ref[...], nw2_ref[...]).astype(BF16)
        acc_sc[...] = jnp.zeros_like(acc_sc)

    u = u_sc[...]
    gate = jnp.dot(u, wg_ref[...], preferred_element_type=F32)
    up = jnp.dot(u, wu_ref[...], preferred_element_type=F32)
    act = (_silu(gate) * up).astype(BF16)
    acc_sc[...] += jnp.dot(act, wd_ref[...], preferred_element_type=F32)

    @pl.when(j == pl.num_programs(1) - 1)
    def _():
        o_ref[...] = h_ref[...] + _rms(acc_sc[...], nw3_ref[...])


def _ffn_layer(h, nw2, wg, wu, wd, nw3, casts=()):
    t = h.shape[0]
    nj = FFN_HIDDEN // TN
    vec = pl.BlockSpec((1, D_MODEL), lambda i, j: (0, 0))
    cast_in, cast_out, cast_shapes = _cast_specs(casts, lambda i, j: i * nj + j)
    in_specs = [
        pl.BlockSpec((TM, D_MODEL), lambda i, j: (i, 0)),
        vec,
        pl.BlockSpec((D_MODEL, TN), lambda i, j: (0, j)),
        pl.BlockSpec((D_MODEL, TN), lambda i, j: (0, j)),
        pl.BlockSpec((TN, D_MODEL), lambda i, j: (j, 0)),
        vec,
    ]
    return pl.pallas_call(
        _with_casts(_ffn_kernel, len(in_specs), 1, len(casts)),
        grid=(t // TM, nj),
        in_specs=in_specs + cast_in,
        out_specs=[pl.BlockSpec((TM, D_MODEL), lambda i, j: (i, 0))] + cast_out,
        out_shape=[jax.ShapeDtypeStruct((t, D_MODEL), F32)] + cast_shapes,
        scratch_shapes=[pltpu.VMEM((TM, D_MODEL), BF16), pltpu.VMEM((TM, D_MODEL), F32)],
        compiler_params=_params("arbitrary", "arbitrary"),
        name="ffn_layer",
    )(h, nw2, wg, wu, wd, nw3, *[w for w, _, _, _ in casts])


def _pad_lanes(v):
    return jnp.pad(v, ((0, 0), (0, LANES - v.shape[1])))


def kernel(x, meta_tokens, norm_w, ssd_w_in, ssd_conv_w, ssd_conv_b, ssd_dt_bias, ssd_a_log, ssd_d,
           ssd_norm_w, ssd_w_out, pool_w, pool_b, pool_scale, ffn_w_gate, ffn_w_up, ffn_w_down):
    bsz, seq, _ = x.shape
    seq_len = N_META + seq
    assert seq_len == TILES_PER_ROW * TM and TM % HALO == 0
    depth = norm_w.shape[0]
    meta = jnp.broadcast_to(meta_tokens[None].astype(x.dtype), (bsz, N_META, D_MODEL))
    h = jnp.concatenate([meta, x], axis=1).reshape(bsz * seq_len, D_MODEL)

    def ffn_casts(layer, chunks_in, chunks_down):
        return [(ffn_w_gate, layer, D_MODEL, chunks_in), (ffn_w_up, layer, D_MODEL, chunks_in),
                (ffn_w_down, layer, FFN_HIDDEN, chunks_down)]

    w_in_t = jnp.swapaxes(ssd_w_in, 1, 2)
    w_in_bf = _cast_alone((w_in_t, 0, ZX_DIM, 32))
    ffn_bf = None
    for i in range(depth):
        j = i // 2
        nw = norm_w[i][:, None, :]
        last = i + 1 == depth
        if i % 2 == 0:
            zx, dt_raw = _inproj(h, nw[0], w_in_bf, w_in_t, j, ssd_conv_w[j], ssd_conv_b[j][None])
            y, w_out_bf, *ffn_bf = _ssd_scan(
                zx, dt_raw, _pad_lanes(ssd_dt_bias[j][None]), _pad_lanes(ssd_a_log[j][None]),
                jnp.repeat(ssd_d[j], SSD_HEAD_DIM)[None], ssd_norm_w[j][None], bsz, seq_len,
                casts=[(ssd_w_out, j, D_INNER, 32)] + ffn_casts(i, 32, 32))
            h = _outproj(y.reshape(bsz * seq_len, D_INNER), w_out_bf, h, nw[1])
            hosted = [] if last else ffn_casts(i + 1, 128, 88)
            h, *ffn_next = _ffn_layer(h, nw[2], *ffn_bf, nw[3], casts=hosted)
            ffn_bf = ffn_next
        else:
            h = _pool_layer(h, nw[0], pool_w[j].astype(BF16), pool_b[j][None],
                            pool_scale[j][None], nw[1])
            hosted = [] if last else [(w_in_t, j + 1, ZX_DIM, 128)]
            h, *w_in_next = _ffn_layer(h, nw[2], *ffn_bf, nw[3], casts=hosted)
            if w_in_next:
                w_in_bf = w_in_next[0]
    return h.reshape(bsz, seq_len, D_MODEL)[:, N_META:]
```
